```python
import math
import jax, jax.numpy as jnp
from jax import lax
import numpy as np

D_MODEL = 2048
BATCH = 1
SEQ = 8192
DEPTH = 1

GRID_W = 64
CTX_LEN = 256
EPS = 1e-6
N_MOD = 6

HY_WIDTH = D_MODEL // 2
FILT_EMB = 33
FILT_BANDS = (FILT_EMB - 1) // 2
FILT_ORDER = 64
FILT_TARGET = 1e-2
FILT_DECAY_FAST = 0.3
FILT_DECAY_SLOW = 1.5
FILT_MIN_DECAY = math.log(FILT_TARGET) / FILT_DECAY_SLOW
FILT_MAX_DECAY = math.log(FILT_TARGET) / FILT_DECAY_FAST
FILT_SHIFT = 0.05

DA_HEADS = 8
DA_HEAD_DIM = 64
DA_V_DIM = 2 * DA_HEAD_DIM
DA_WIDTH = DA_HEADS * DA_V_DIM
ROPE_BASE = 10000.0
ROPE_AXIS_DIM = DA_HEAD_DIM // 2
Q_BLOCK = 128

PEER_HEADS = 8
PEER_N_KEYS = 128
PEER_N_EXPERTS = PEER_N_KEYS * PEER_N_KEYS
PEER_TOPK = 16
PEER_DK = 256
PEER_DK_HALF = PEER_DK // 2
PEER_BLOCK = 128

DA_Q_OFF = 3 * HY_WIDTH
DA_K_OFF = DA_Q_OFF + DA_HEADS * 2 * DA_HEAD_DIM
DA_V_OFF = DA_K_OFF + DA_HEADS * 2 * DA_HEAD_DIM
GATE_OFF = DA_V_OFF + DA_WIDTH
IN_COLS = GATE_OFF + 2 * D_MODEL

kernel_name = "hyena_diffattn_peer_hybrid_block"


def rmsnorm(x, g):
    xf = x.astype(jnp.float32)
    y = xf * lax.rsqrt(jnp.mean(xf * xf, axis=-1, keepdims=True) + EPS)
    return (y * g.astype(jnp.float32)).astype(x.dtype)


def modulate(h, shift, scale):
    return h * (1 + scale) + shift


def short_conv(u, w, b):
    up = jnp.pad(u, ((0, 0), (1, 1), (0, 0)))
    return up[:, :-2] * w[0] + up[:, 1:-1] * w[1] + up[:, 2:] * w[2] + b


def hyena_filter_spectrum(L, f_w1, f_b1, f_w2, f_b2, f_w3, f_b3, f_wout, f_freq):
    f32 = jnp.float32
    t = jnp.linspace(0.0, 1.0, L, dtype=f32)[:, None]
    w = 2.0 * math.pi * jnp.arange(L, dtype=f32)[:, None] / L
    bands = jnp.linspace(1e-4, FILT_BANDS - 1, FILT_BANDS, dtype=f32)[None, :]
    z = jnp.concatenate([t, jnp.cos(bands * w), -jnp.sin(bands * w)], axis=-1).astype(f_w1.dtype)
    h = jnp.sin(f_freq * (z @ f_w1 + f_b1))
    h = jnp.sin(f_freq * (h @ f_w2 + f_b2))
    h = jnp.sin(f_freq * (h @ f_w3 + f_b3))
    h = (h @ f_wout).astype(f32)
    deltas = jnp.abs(jnp.linspace(FILT_MIN_DECAY, FILT_MAX_DECAY, HY_WIDTH, dtype=f32))
    window = jnp.exp(-t * deltas[None, :]) + FILT_SHIFT
    h_fwd = h[:, :HY_WIDTH] * window
    h_bwd = h[:, HY_WIDTH:] * window
    k = jnp.concatenate([h_fwd, jnp.zeros((1, HY_WIDTH), f32), h_bwd[1:][::-1]], axis=0)
    return jnp.fft.rfft(k, axis=0)


def fft_long_conv(u, k_spec, bias):
    L = u.shape[1]
    uf = jnp.fft.rfft(u.astype(jnp.float32), n=2 * L, axis=1)
    y = jnp.fft.irfft(uf * k_spec[None], n=2 * L, axis=1)[:, :L]
    return (y + u.astype(jnp.float32) * bias.astype(jnp.float32)).astype(u.dtype)


def hyena_mix(p_hy, conv_w, conv_b, f_w1, f_b1, f_w2, f_b2, f_w3, f_b3, f_wout, f_freq, bias):
    L = p_hy.shape[1]
    z = short_conv(p_hy, conv_w, conv_b)
    v, x1, x0 = jnp.split(z, 3, axis=-1)
    k_spec = hyena_filter_spectrum(L, f_w1, f_b1, f_w2, f_b2, f_w3, f_b3, f_wout, f_freq)
    return x0 * fft_long_conv(v * x1, k_spec, bias)


def split_qkv(p):
    B, L, _ = p.shape
    q = p[..., DA_Q_OFF:DA_K_OFF].reshape(B, L, DA_HEADS, 2, DA_HEAD_DIM)
    k = p[..., DA_K_OFF:DA_V_OFF].reshape(B, L, DA_HEADS, 2, DA_HEAD_DIM)
    v = p[..., DA_V_OFF:GATE_OFF].reshape(B, L, DA_HEADS, DA_V_DIM)
    return q, k, v


def rope_axis(xa, pos):
    half = ROPE_AXIS_DIM // 2
    inv = ROPE_BASE ** (-jnp.arange(0, half, dtype=jnp.float32) / half)
    ang = pos.astype(jnp.float32)[:, None] * inv[None, :]
    cos = jnp.cos(ang)[None, :, None, None, :].astype(xa.dtype)
    sin = jnp.sin(ang)[None, :, None, None, :].astype(xa.dtype)
    x1, x2 = xa[..., :half], xa[..., half:]
    return jnp.concatenate([x1 * cos - x2 * sin, x2 * cos + x1 * sin], axis=-1)


def rope2d(x, row_pos, col_pos):
    return jnp.concatenate([rope_axis(x[..., :ROPE_AXIS_DIM], row_pos),
                            rope_axis(x[..., ROPE_AXIS_DIM:], col_pos)], axis=-1)


def diff_attend(q, k, v, lam):
    s = jnp.einsum('bqhcd,bkhcd->bhcqk', q, k).astype(jnp.float32) * (DA_HEAD_DIM ** -0.5)
    p = jax.nn.softmax(s, axis=-1)
    a = (p[:, :, 0] - lam * p[:, :, 1]).astype(v.dtype)
    return jnp.einsum('bhqk,bkhe->bqhe', a, v)


def diff_head_out(o, subln, lam_init):
    B, L = o.shape[:2]
    return (rmsnorm(o, subln) * (1.0 - lam_init)).reshape(B, L, DA_WIDTH)


def merge_branches(y_hy, y_da, gates, w_hy_out, w_da_out, w_o):
    g_hy, g_da = jnp.split(gates, 2, axis=-1)
    m = jax.nn.sigmoid(g_hy) * (y_hy @ w_hy_out) + jax.nn.sigmoid(g_da) * (y_da @ w_da_out)
    return m @ w_o


def peer(h, w_q, keys, u_tab, v_tab):
    B, L, D = h.shape
    hb = h.reshape(-1, PEER_BLOCK, D)

    def block(hx):
        n = hx.shape[0]
        q = (hx @ w_q).reshape(n, PEER_HEADS, 2, PEER_DK_HALF)
        s = jnp.einsum('nhpd,phkd->nhpk', q, keys)
        sv, si = lax.top_k(s, PEER_TOPK)
        cand_s = (sv[:, :, 0, :, None] + sv[:, :, 1, None, :]).reshape(n, PEER_HEADS, PEER_TOPK * PEER_TOPK)
        cand_i = (si[:, :, 0, :, None] * PEER_N_KEYS + si[:, :, 1, None, :]).reshape(n, PEER_HEADS, PEER_TOPK * PEER_TOPK)
        top_s, top_p = lax.top_k(cand_s, PEER_TOPK)
        idx = jnp.take_along_axis(cand_i, top_p, axis=-1)
        g = jax.nn.softmax(top_s.astype(jnp.float32), axis=-1)
        act = jax.nn.gelu(jnp.einsum('nd,nhkd->nhk', hx, u_tab[idx]), approximate=False)
        wgt = (g * act.astype(jnp.float32)).astype(hx.dtype)
        return jnp.einsum('nhk,nhkd->nd', wgt, v_tab[idx])

    return lax.map(block, hb).reshape(B, L, D)


def setup_inputs(seed: int = 0) -> dict:
    key = jax.random.key(seed)
    ks = jax.random.split(key, 32)
    f32 = jnp.float32
    D, C = D_MODEL, HY_WIDTH

    def nrm(k, shape, scale):
        return jax.random.normal(k, shape, f32) * scale

    return {
        "x": nrm(ks[0], (BATCH, SEQ, D), 1.0),
        "c": nrm(ks[1], (BATCH, D), 1.0),
        "ctx": nrm(ks[2], (BATCH, CTX_LEN, D), 1.0),
        "c_ctx": nrm(ks[3], (D,), 1.0),
        "w_mod": nrm(ks[4], (DEPTH, D, N_MOD * D), 0.5 * D ** -0.5),
        "b_mod": nrm(ks[5], (DEPTH, N_MOD * D), 0.01),
        "norm_pre_mix": 1.0 + nrm(ks[6], (DEPTH, D), 0.05),
        "norm_post_mix": 1.0 + nrm(ks[7], (DEPTH, D), 0.05),
        "norm_pre_ffn": 1.0 + nrm(ks[8], (DEPTH, D), 0.05),
        "norm_post_ffn": 1.0 + nrm(ks[9], (DEPTH, D), 0.05),
        "w_in": nrm(ks[10], (DEPTH, D, IN_COLS), D ** -0.5),
        "hy_conv_w": nrm(ks[11], (DEPTH, 3, 3 * C), 3 ** -0.5),
        "hy_conv_b": nrm(ks[12], (DEPTH, 3 * C), 0.01),
        "filt_w1": nrm(ks[13], (DEPTH, FILT_EMB, FILT_ORDER), FILT_EMB ** -0.5),
        "filt_b1": nrm(ks[14], (DEPTH, FILT_ORDER), 0.1),
        "filt_w2": nrm(ks[15], (DEPTH, FILT_ORDER, FILT_ORDER), FILT_ORDER ** -0.5),
        "filt_b2": nrm(ks[16], (DEPTH, FILT_ORDER), 0.1),
        "filt_w3": nrm(ks[17], (DEPTH, FILT_ORDER, FILT_ORDER), FILT_ORDER ** -0.5),
        "filt_b3": nrm(ks[18], (DEPTH, FILT_ORDER), 0.1),
        "filt_w_out": nrm(ks[19], (DEPTH, FILT_ORDER, 2 * C), 0.1 * FILT_ORDER ** -0.5),
        "filt_freq": 1.0 + nrm(ks[20], (DEPTH, FILT_ORDER), 0.05),
        "hy_bias": nrm(ks[21], (DEPTH, C), 0.1),
        "da_lambda": nrm(ks[22], (DEPTH, 4, DA_HEAD_DIM), 0.1),
        "da_subln": 1.0 + nrm(ks[23], (DEPTH, DA_V_DIM), 0.05),
        "w_hy_out": nrm(ks[24], (DEPTH, C, D), C ** -0.5),
        "w_da_out": nrm(ks[25], (DEPTH, DA_WIDTH, D), DA_WIDTH ** -0.5),
        "w_o": nrm(ks[26], (DEPTH, D, D), D ** -0.5),
        "peer_w_q": nrm(ks[27], (DEPTH, D, PEER_HEADS * PEER_DK), D ** -0.5),
        "peer_keys": nrm(ks[28], (DEPTH, 2, PEER_HEADS, PEER_N_KEYS, PEER_DK_HALF), PEER_DK_HALF ** -0.5),
        "peer_u": nrm(ks[29], (DEPTH, PEER_N_EXPERTS, D), D ** -0.5),
        "peer_v": nrm(ks[30], (DEPTH, PEER_N_EXPERTS, D), D ** -0.5),
    }


def reference(x, c, ctx, c_ctx, w_mod, b_mod, norm_pre_mix, norm_post_mix, norm_pre_ffn, norm_post_ffn,
              w_in, hy_conv_w, hy_conv_b, filt_w1, filt_b1, filt_w2, filt_b2, filt_w3, filt_b3,
              filt_w_out, filt_freq, hy_bias, da_lambda, da_subln, w_hy_out, w_da_out, w_o,
              peer_w_q, peer_keys, peer_u, peer_v):
    B, T, D = x.shape
    rows = T // GRID_W
    row_pos = jnp.repeat(jnp.arange(rows, dtype=jnp.int32), GRID_W)
    col_pos = jnp.tile(jnp.arange(GRID_W, dtype=jnp.int32), rows)
    n_blocks = T // Q_BLOCK
    cx = ctx
    for l in range(DEPTH):
        last = l == DEPTH - 1
        mod_lat = (jax.nn.silu(c) @ w_mod[l] + b_mod[l]).reshape(B, 1, N_MOD, D)
        mod_ctx = (jax.nn.silu(c_ctx) @ w_mod[l] + b_mod[l]).reshape(1, 1, N_MOD, D)
        sh1, sc1, g1, sh2, sc2, g2 = [mod_lat[:, :, i] for i in range(N_MOD)]
        csh1, csc1, cg1, csh2, csc2, cg2 = [mod_ctx[:, :, i] for i in range(N_MOD)]
        hy_args = (hy_conv_w[l], hy_conv_b[l], filt_w1[l], filt_b1[l], filt_w2[l], filt_b2[l],
                   filt_w3[l], filt_b3[l], filt_w_out[l], filt_freq[l], hy_bias[l])
        lam_init = 0.8 - 0.6 * math.exp(-0.3 * l)
        lam_p = da_lambda[l].astype(jnp.float32)
        lam = jnp.exp(jnp.sum(lam_p[0] * lam_p[1])) - jnp.exp(jnp.sum(lam_p[2] * lam_p[3])) + lam_init

        u_lat = modulate(rmsnorm(x, norm_pre_mix[l]), sh1, sc1)
        u_ctx = modulate(rmsnorm(cx, norm_pre_mix[l]), csh1, csc1)
        p_lat = u_lat @ w_in[l]
        p_ctx = u_ctx @ w_in[l]

        y_hy = hyena_mix(p_lat[..., :DA_Q_OFF], *hy_args)
        q_lat, k_lat, v_lat = split_qkv(p_lat)
        q_lat = rope2d(q_lat, row_pos, col_pos)
        k_lat = rope2d(k_lat, row_pos, col_pos)
        q_ctx, k_ctx, v_ctx = split_qkv(p_ctx)
        k_all = jnp.concatenate([k_ctx, k_lat], axis=1)
        v_all = jnp.concatenate([v_ctx, v_lat], axis=1)
        q_blocks = q_lat.reshape(B, n_blocks, Q_BLOCK, DA_HEADS, 2, DA_HEAD_DIM).swapaxes(0, 1)
        o_blocks = lax.map(lambda qb: diff_attend(qb, k_all, v_all, lam), q_blocks)
        o_lat = o_blocks.swapaxes(0, 1).reshape(B, T, DA_HEADS, DA_V_DIM)
        y_da = diff_head_out(o_lat, da_subln[l], lam_init)
        mix_lat = merge_branches(y_hy, y_da, p_lat[..., GATE_OFF:], w_hy_out[l], w_da_out[l], w_o[l])
        x = x + g1 * rmsnorm(mix_lat, norm_post_mix[l])
        if not last:
            y_hy_c = hyena_mix(p_ctx[..., :DA_Q_OFF], *hy_args)
            y_da_c = diff_head_out(diff_attend(q_ctx, k_ctx, v_ctx, lam), da_subln[l], lam_init)
            mix_ctx = merge_branches(y_hy_c, y_da_c, p_ctx[..., GATE_OFF:], w_hy_out[l], w_da_out[l], w_o[l])
            cx = cx + cg1 * rmsnorm(mix_ctx, norm_post_mix[l])

        h_lat = modulate(rmsnorm(x, norm_pre_ffn[l]), sh2, sc2)
        x = x + g2 * rmsnorm(peer(h_lat, peer_w_q[l], peer_keys[l], peer_u[l], peer_v[l]), norm_post_ffn[l])
        if not last:
            h_ctx = modulate(rmsnorm(cx, norm_pre_ffn[l]), csh2, csc2)
            cx = cx + cg2 * rmsnorm(peer(h_ctx, peer_w_q[l], peer_keys[l], peer_u[l], peer_v[l]), norm_post_ffn[l])
    return x
```

```python
import functools
import math

import numpy as np
import jax
import jax.numpy as jnp
from jax import lax
from jax.experimental import pallas as pl
from jax.experimental.pallas import tpu as pltpu

F32 = jnp.float32
BF16 = jnp.bfloat16

EPS = 1e-6
GRID_W = 64
N_MOD = 6
LANES = 128

FILT_EMB = 33
FILT_BANDS = (FILT_EMB - 1) // 2
FILT_EMB_PAD = 64
FILT_TARGET = 1e-2
FILT_DECAY_FAST = 0.3
FILT_DECAY_SLOW = 1.5
FILT_MIN_DECAY = math.log(FILT_TARGET) / FILT_DECAY_SLOW
FILT_MAX_DECAY = math.log(FILT_TARGET) / FILT_DECAY_FAST
FILT_SHIFT = 0.05

DA_HEAD_DIM = 64
DA_V_DIM = 2 * DA_HEAD_DIM
ROPE_BASE = 10000.0
ROPE_AXIS_DIM = DA_HEAD_DIM // 2
LAM_INIT = 0.8 - 0.6 * math.exp(-0.3 * 0)

PEER_TOPK = 16
NEG_INF = float("-inf")

VMEM_LIMIT = 56 * 1024 * 1024


def _split(x):
    x = np.asarray(x, np.float64)
    hi = x.astype(np.float32)
    lo = (x - hi.astype(np.float64)).astype(np.float32)
    return hi, lo


def _times(v, hilo):
    return v * hilo[0] + v * hilo[1]


FILT_SHIFT_HL = tuple(float(a) for a in _split(FILT_SHIFT))
LAM_INIT_HL = tuple(float(a) for a in _split(LAM_INIT))
ONE_MINUS_LAM_INIT_HL = tuple(float(a) for a in _split(1.0 - LAM_INIT))
Q_SCALE_HL = tuple(float(a) for a in _split(DA_HEAD_DIM ** -0.5 * math.log2(math.e)))


def _params(*sem):
    return pltpu.CompilerParams(dimension_semantics=sem, vmem_limit_bytes=VMEM_LIMIT)


def _pick(n, pref):
    if n <= pref:
        return n
    t = pref
    while n % t:
        t //= 2
    return t


def _mod_kernel(c_ref, w_ref, b_ref, o_ref):
    c = c_ref[...]
    s = c * jax.nn.sigmoid(c)
    w = w_ref[...]
    rows = [jnp.sum(s[r] * w, axis=0, keepdims=True) for r in range(2)]
    o_ref[...] = jnp.concatenate(rows, axis=0) + b_ref[...]


def _modulation(c2, w_mod, b_mod):
    D, N = w_mod.shape
    tn = _pick(N, 512)
    return pl.pallas_call(
        _mod_kernel,
        grid=(N // tn,),
        in_specs=[pl.BlockSpec((2, D, 1), lambda j: (0, 0, 0)),
                  pl.BlockSpec((D, tn), lambda j: (0, j)),
                  pl.BlockSpec((1, tn), lambda j: (0, j))],
        out_specs=pl.BlockSpec((2, tn), lambda j: (0, j)),
        out_shape=jax.ShapeDtypeStruct((2, N), F32),
        compiler_params=_params("parallel"),
        name="mod",
    )(c2.reshape(2, D, 1), w_mod, b_mod.reshape(1, N))


def _proj_kernel(x_ref, nw_ref, sh_ref, sc_ref, w_ref, o_ref, u_scr):
    @pl.when(pl.program_id(1) == 0)
    def _():
        xf = x_ref[...]
        y = xf * lax.rsqrt(jnp.mean(xf * xf, axis=-1, keepdims=True) + EPS) * nw_ref[...]
        u_scr[...] = (y * (1.0 + sc_ref[...]) + sh_ref[...]).astype(BF16)

    o_ref[...] = jnp.dot(u_scr[...], w_ref[...], preferred_element_type=F32).astype(o_ref.dtype)


def _project(x, norm_w, shift, scale, w_bf16, col0, ncols, tm_pref=512, tn_pref=1024):
    M, D = x.shape
    tm = _pick(M, tm_pref)
    tn = _pick(ncols, tn_pref)
    assert col0 % tn == 0
    cb = col0 // tn
    vec = pl.BlockSpec((1, D), lambda i, j: (0, 0))
    return pl.pallas_call(
        _proj_kernel,
        grid=(M // tm, ncols // tn),
        in_specs=[pl.BlockSpec((tm, D), lambda i, j: (i, 0)), vec, vec, vec,
                  pl.BlockSpec((D, tn), lambda i, j: (0, j + cb))],
        out_specs=pl.BlockSpec((tm, tn), lambda i, j: (i, j)),
        out_shape=jax.ShapeDtypeStruct((M, ncols), BF16),
        scratch_shapes=[pltpu.VMEM((tm, D), BF16)],
        compiler_params=_params("parallel", "arbitrary"),
        name="proj",
    )(x, norm_w, shift, scale, w_bf16)


def _rope_tables(T):
    half = ROPE_AXIS_DIM // 2
    lane = np.arange(LANES)
    d64 = lane % DA_HEAD_DIM
    axis = d64 // ROPE_AXIS_DIM
    d32 = d64 % ROPE_AXIS_DIM
    first = d32 < half
    idx = np.where(first, d32, d32 - half)
    inv_hi, inv_lo = _split((ROPE_BASE ** (-np.arange(half, dtype=np.float64) / half))[idx][None, :])
    t = lax.broadcasted_iota(jnp.int32, (T, 1), 0)
    pos = jnp.where(jnp.asarray(axis[None, :] == 0), t // GRID_W, t % GRID_W).astype(F32)
    ang = _times(pos, (inv_hi, inv_lo))
    cos = jnp.cos(ang)
    sin = jnp.sin(ang) * jnp.asarray(np.where(first, -1.0, 1.0)[None, :], F32)
    partner = np.where(first, lane + half, lane - half)
    perm = np.zeros((LANES, LANES), np.float32)
    perm[partner, lane] = 1.0
    return cos, sin, jnp.asarray(perm, BF16)


def _rope_kernel(pq_ref, pk_ref, pv_ref, pc_ref, cos_ref, sin_ref, perm_ref, q_ref, k_ref, v_ref, *, heads):
    s = pl.program_id(0)
    hq = heads * LANES

    def rope(xb):
        rot = jnp.dot(xb, perm_ref[...], preferred_element_type=F32)
        return xb.astype(F32) * cos_ref[...] + rot * sin_ref[...]

    for h in range(heads):
        sl = slice(h * LANES, (h + 1) * LANES)
        q_ref[:, sl] = _times(rope(pq_ref[:, sl]), Q_SCALE_HL).astype(BF16)

    @pl.when(s == 0)
    def _():
        k_ref[...] = pc_ref[:, :hq]
        v_ref[...] = pc_ref[:, hq:]

    @pl.when(s > 0)
    def _():
        for h in range(heads):
            sl = slice(h * LANES, (h + 1) * LANES)
            k_ref[:, sl] = rope(pk_ref[:, sl]).astype(BF16)
        v_ref[...] = pv_ref[...]


def _rope_assemble(p, p_ctx, heads, q_off):
    T = p.shape[0]
    ctx = p_ctx.shape[0]
    hq = heads * LANES
    assert q_off % hq == 0 and T % ctx == 0
    qb = q_off // hq
    cos, sin, perm = _rope_tables(T)
    lat = lambda c: pl.BlockSpec((ctx, hq), lambda s, c=c: (jnp.maximum(s - 1, 0), c))
    tab = pl.BlockSpec((ctx, LANES), lambda s: (jnp.maximum(s - 1, 0), 0))
    return pl.pallas_call(
        functools.partial(_rope_kernel, heads=heads),
        grid=(T // ctx + 1,),
        in_specs=[lat(qb), lat(qb + 1), lat(qb + 2),
                  pl.BlockSpec((ctx, 2 * hq), lambda s: (0, 0)), tab, tab,
                  pl.BlockSpec((LANES, LANES), lambda s: (0, 0))],
        out_specs=[pl.BlockSpec((ctx, hq), lambda s: (jnp.maximum(s - 1, 0), 0)),
                   pl.BlockSpec((ctx, hq), lambda s: (s, 0)),
                   pl.BlockSpec((ctx, hq), lambda s: (s, 0))],
        out_shape=[jax.ShapeDtypeStruct((T, hq), BF16),
                   jax.ShapeDtypeStruct((T + ctx, hq), BF16),
                   jax.ShapeDtypeStruct((T + ctx, hq), BF16)],
        compiler_params=_params("arbitrary"),
        name="rope",
    )(p, p, p, p_ctx, cos, sin, perm)


def _attn_kernel(lam_ref, sub_ref, q_ref, k_ref, v_ref, o_ref, m_scr, l_scr, acc_scr, s_scr, p_scr, *, tk, rc):
    tq = q_ref.shape[0]
    nk = k_ref.shape[0] // tk
    q = q_ref[...]
    lane = lax.broadcasted_iota(jnp.int32, q.shape, 1)
    zero = jnp.zeros_like(q)
    qs = (jnp.where(lane < DA_HEAD_DIM, q, zero), jnp.where(lane >= DA_HEAD_DIM, q, zero))
    m_scr[...] = jnp.full(m_scr.shape, NEG_INF, F32)
    l_scr[...] = jnp.zeros(l_scr.shape, F32)
    acc_scr[...] = jnp.zeros(acc_scr.shape, F32)

    def body(j, carry):
        start = pl.multiple_of(j * tk, tk)
        k = k_ref[pl.ds(start, tk), :]
        v = v_ref[pl.ds(start, tk), :]
        for c in range(2):
            s_scr[c] = lax.dot_general(qs[c], k, (((1,), (1,)), ((), ())), preferred_element_type=F32)
        for c in range(2):
            for r in range(tq // rc):
                rows = slice(r * rc, (r + 1) * rc)
                s = s_scr[c, rows, :]
                m_old = m_scr[c, rows, :]
                m_new = jnp.maximum(m_old, jnp.max(s, axis=-1, keepdims=True))
                alpha = jnp.exp2(m_old - m_new)
                p = jnp.exp2(s - pltpu.repeat(m_new, tk // LANES, axis=1))
                l_scr[c, rows, :] = alpha * l_scr[c, rows, :] + jnp.sum(p, axis=-1, keepdims=True)
                acc_scr[c, rows, :] = alpha * acc_scr[c, rows, :]
                m_scr[c, rows, :] = m_new
                p_scr[c, rows, :] = p.astype(BF16)
            acc_scr[c] += jnp.dot(p_scr[c], v, preferred_element_type=F32)
        return carry

    lax.fori_loop(0, nk, body, 0)

    lp = lam_ref[...]
    lam = (jnp.exp(jnp.sum(lp[0:1] * lp[1:2], axis=-1, keepdims=True))
           - jnp.exp(jnp.sum(lp[2:3] * lp[3:4], axis=-1, keepdims=True)) + LAM_INIT_HL[0] + LAM_INIT_HL[1])
    o = acc_scr[0] / l_scr[0] - lam * (acc_scr[1] / l_scr[1])
    y = o * lax.rsqrt(jnp.mean(o * o, axis=-1, keepdims=True) + EPS) * sub_ref[...]
    o_ref[...] = _times(y, ONE_MINUS_LAM_INIT_HL).astype(o_ref.dtype)


ATTN_ROW_CHUNK = 32


def _kv_tile(n, pref=1408):
    best = LANES
    for t in range(LANES, pref + 1, LANES):
        if n % t == 0:
            best = t
    return best


def _diff_attention(q, k, v, lam_p, subln, heads):
    T = q.shape[0]
    Tk = k.shape[0]
    tq = _pick(T, 1024)
    tk = _kv_tile(Tk)
    kv = pl.BlockSpec((Tk, LANES), lambda h, i: (0, h))
    return pl.pallas_call(
        functools.partial(_attn_kernel, tk=tk, rc=min(ATTN_ROW_CHUNK, tq)),
        grid=(heads, T // tq),
        in_specs=[pl.BlockSpec(lam_p.shape, lambda h, i: (0, 0)),
                  pl.BlockSpec((1, LANES), lambda h, i: (0, 0)),
                  pl.BlockSpec((tq, LANES), lambda h, i: (i, h)), kv, kv],
        out_specs=pl.BlockSpec((tq, LANES), lambda h, i: (i, h)),
        out_shape=jax.ShapeDtypeStruct((T, heads * LANES), BF16),
        scratch_shapes=[pltpu.VMEM((2, tq, LANES), F32), pltpu.VMEM((2, tq, LANES), F32),
                        pltpu.VMEM((2, tq, LANES), F32), pltpu.VMEM((2, tq, tk), F32),
                        pltpu.VMEM((2, tq, tk), BF16)],
        compiler_params=_params("parallel", "arbitrary"),
        name="attn",
    )(lam_p, subln.reshape(1, LANES), q, k, v)


def _filter_features(L):
    n = lax.broadcasted_iota(jnp.int32, (2 * L, 1), 0)
    tt = jnp.where(n < L, n, 2 * L - n).astype(F32)
    t = _times(tt, _split(1.0 / (L - 1)))
    w = _times(tt, _split(2.0 * math.pi / L))
    b_hi, b_lo = _split(np.linspace(1e-4, FILT_BANDS - 1, FILT_BANDS, dtype=np.float64)[None, :])
    ang = (jnp.asarray(b_hi) + jnp.asarray(b_lo)) * w
    z = jnp.concatenate([t, jnp.cos(ang), -jnp.sin(ang),
                         jnp.zeros((2 * L, FILT_EMB_PAD - FILT_EMB), F32)], axis=-1)
    return z, t


def _filter_kernel(z_ref, t_ref, w1_ref, b1_ref, w2_ref, b2_ref, w3_ref, b3_ref, wo_ref, fr_ref, dl_ref,
                   o_ref, *, L):
    tr = z_ref.shape[0]
    fr = fr_ref[...]
    h = jnp.sin(fr * (jnp.dot(z_ref[...], w1_ref[...], preferred_element_type=F32) + b1_ref[...]))
    h = jnp.sin(fr * (jnp.dot(h, w2_ref[...], preferred_element_type=F32) + b2_ref[...]))
    h = jnp.sin(fr * (jnp.dot(h, w3_ref[...], preferred_element_type=F32) + b3_ref[...]))
    h = jnp.dot(h, wo_ref[...], preferred_element_type=F32)
    window = jnp.exp(-t_ref[...] * dl_ref[...]) + FILT_SHIFT_HL[0] + FILT_SHIFT_HL[1]
    row = pl.program_id(0) * tr + lax.broadcasted_iota(jnp.int32, (tr, 1), 0)
    o_ref[...] = jnp.where(row == L, 0.0, h * window).astype(o_ref.dtype)


def _hyena_filter(L, C, f_w1, f_b1, f_w2, f_b2, f_w3, f_b3, f_wout, f_freq):
    z, t = _filter_features(L)
    order = f_w1.shape[1]
    w1 = jnp.concatenate([f_w1, jnp.zeros((FILT_EMB_PAD - FILT_EMB, order), F32)], axis=0)
    d_hi, d_lo = _split(np.abs(np.linspace(FILT_MIN_DECAY, FILT_MAX_DECAY, C, dtype=np.float64))[None, :])
    deltas = jnp.asarray(d_hi) + jnp.asarray(d_lo)
    tr = _pick(L, 512)
    nb = L // tr
    full = lambda a: pl.BlockSpec(a.shape, lambda i: (0,) * a.ndim)
    row = lambda a: a.reshape(1, -1)
    args = (w1, row(f_b1), f_w2, row(f_b2), f_w3, row(f_b3))
    return pl.pallas_call(
        functools.partial(_filter_kernel, L=L),
        grid=(2 * nb,),
        in_specs=[pl.BlockSpec((tr, FILT_EMB_PAD), lambda i: (i, 0)), pl.BlockSpec((tr, 1), lambda i: (i, 0))]
                 + [full(a) for a in args]
                 + [pl.BlockSpec((order, C), lambda i: (0, i // nb)), full(row(f_freq)), full(deltas)],
        out_specs=pl.BlockSpec((tr, C), lambda i: (i, 0)),
        out_shape=jax.ShapeDtypeStruct((2 * L, C), BF16),
        compiler_params=_params("parallel"),
        name="filter",
    )(z, t, *args, f_wout, row(f_freq), deltas)


def _sconv_kernel(p_ref, prev_ref, next_ref, w_ref, b_ref, u_ref, x0_ref):
    i = pl.program_id(0)
    tm, c3 = p_ref.shape
    C = c3 // 3
    halo = prev_ref.shape[0]
    x = p_ref[...].astype(F32)
    prev_row = jnp.where(i > 0, prev_ref[halo - 1:halo, :].astype(F32), 0.0)
    next_row = jnp.where(i < pl.num_programs(0) - 1, next_ref[0:1, :].astype(F32), 0.0)
    rows = lax.broadcasted_iota(jnp.int32, (tm, 1), 0)
    up = jnp.where(rows == 0, prev_row, pltpu.roll(x, 1, axis=0))
    dn = jnp.where(rows == tm - 1, next_row, pltpu.roll(x, tm - 1, axis=0))
    w = w_ref[...]
    z = up * w[0:1] + x * w[1:2] + dn * w[2:3] + b_ref[...]
    u_ref[...] = (z[:, :C] * z[:, C:2 * C]).astype(u_ref.dtype)
    x0_ref[...] = z[:, 2 * C:].astype(x0_ref.dtype)


def _short_conv_gate(p, C, conv_w, conv_b):
    T = p.shape[0]
    tm = _pick(T, 256)
    halo = 16
    r = tm // halo
    nh = T // halo
    return pl.pallas_call(
        _sconv_kernel,
        grid=(T // tm,),
        in_specs=[pl.BlockSpec((tm, 3 * C), lambda i: (i, 0)),
                  pl.BlockSpec((halo, 3 * C), lambda i: (jnp.maximum(i * r - 1, 0), 0)),
                  pl.BlockSpec((halo, 3 * C), lambda i: (jnp.minimum((i + 1) * r, nh - 1), 0)),
                  pl.BlockSpec((3, 3 * C), lambda i: (0, 0)),
                  pl.BlockSpec((1, 3 * C), lambda i: (0, 0))],
        out_specs=[pl.BlockSpec((tm, C), lambda i: (i, 0)), pl.BlockSpec((tm, C), lambda i: (i, 0))],
        out_shape=[jax.ShapeDtypeStruct((T, C), BF16), jax.ShapeDtypeStruct((T, C), BF16)],
        compiler_params=_params("parallel"),
        name="sconv",
    )(p, p, p, conv_w, conv_b.reshape(1, -1))


def _dft_tables(R):
    N = R * R
    a = jnp.arange(R, dtype=jnp.int32)
    th = _times(((a[:, None] * a[None, :]) % R).astype(F32), _split(2.0 * math.pi / R))
    cr, ci = jnp.cos(th), -jnp.sin(th)
    lead = jnp.concatenate([cr, ci], axis=0)
    f = a[:, None, None] + R * a[None, :, None]
    ph = _times(((f * a[None, None, :]) % N).astype(F32), _split(2.0 * math.pi / N))
    gr, gi = jnp.cos(ph), -jnp.sin(ph)
    fwd = jnp.concatenate([gr, gi], axis=1)
    inv = jnp.concatenate([gr.transpose(0, 2, 1), gi.transpose(0, 2, 1)], axis=1)
    out_r = cr[: R // 2] * (1.0 / N)
    out_i = ci[: R // 2] * (1.0 / N)
    cast = lambda x: x.astype(BF16)
    return cast(lead), cast(fwd), cast(inv), cast(out_r), cast(out_i)


def _dft1_kernel(lhs_ref, x_ref, br_ref, bi_ref):
    R = br_ref.shape[0]
    t = jnp.dot(lhs_ref[...], x_ref[...], preferred_element_type=F32)
    br_ref[...] = t[:R].astype(br_ref.dtype)
    bi_ref[...] = t[R:].astype(bi_ref.dtype)


def _dft_lead(lhs, x2d, R):
    K, W = x2d.shape
    tc = _pick(W, 4096)
    out = jax.ShapeDtypeStruct((R, W), BF16)
    return pl.pallas_call(
        _dft1_kernel,
        grid=(W // tc,),
        in_specs=[pl.BlockSpec((2 * R, K), lambda j: (0, 0)), pl.BlockSpec((K, tc), lambda j: (0, j))],
        out_specs=[pl.BlockSpec((R, tc), lambda j: (0, j)), pl.BlockSpec((R, tc), lambda j: (0, j))],
        out_shape=[out, out],
        compiler_params=_params("parallel"),
        name="dft1",
    )(lhs, x2d)


def _spec_kernel(fwd_ref, inv_ref, ur_ref, ui_ref, kr_ref, ki_ref, cr_ref, ci_ref):
    fb, R, _ = ur_ref.shape
    for b in range(fb):
        g = fwd_ref[b]
        dot = lambda m, x: jnp.dot(m, x, preferred_element_type=F32)
        t1, t2 = dot(g, ur_ref[b]), dot(g, ui_ref[b])
        xr, xi = t1[:R] - t2[R:], t1[R:] + t2[:R]
        t1, t2 = dot(g, kr_ref[b]), dot(g, ki_ref[b])
        kr, ki = t1[:R] - t2[R:], t1[R:] + t2[:R]
        yr = (xr * kr - xi * ki).astype(BF16)
        yi = (xr * ki + xi * kr).astype(BF16)
        gt = inv_ref[b]
        t3, t4 = dot(gt, yr), dot(gt, yi)
        cr_ref[b] = (t3[:R] + t4[R:]).astype(cr_ref.dtype)
        ci_ref[b] = (t4[:R] - t3[R:]).astype(ci_ref.dtype)


def _spectral(fwd, inv, bur, bui, bkr, bki, R, C):
    fb = _pick(R, 4)
    dat = pl.BlockSpec((fb, R, C), lambda i: (i, 0, 0))
    tab = pl.BlockSpec((fb, 2 * R, R), lambda i: (i, 0, 0))
    out = jax.ShapeDtypeStruct((R, R, C), BF16)
    r3 = lambda a: a.reshape(R, R, C)
    return pl.pallas_call(
        _spec_kernel,
        grid=(R // fb,),
        in_specs=[tab, tab, dat, dat, dat, dat],
        out_specs=[dat, dat],
        out_shape=[out, out],
        compiler_params=_params("parallel"),
        name="spec",
    )(fwd, inv, r3(bur), r3(bui), r3(bkr), r3(bki))


def _idft_kernel(lr_ref, li_ref, cr_ref, ci_ref, u_ref, x0_ref, bias_ref, y_ref):
    y = (jnp.dot(lr_ref[...], cr_ref[...], preferred_element_type=F32)
         + jnp.dot(li_ref[...], ci_ref[...], preferred_element_type=F32))
    u = u_ref[...].astype(F32)
    y_ref[...] = (x0_ref[...].astype(F32) * (y + u * bias_ref[...])).astype(y_ref.dtype)


def _idft_lead_gate(out_r, out_i, cr, ci, u2d, x02d, bias, R, C):
    W = R * C
    tc = _pick(W, 4096)
    assert tc % C == 0 or C % tc == 0
    bias_t = jnp.tile(bias.reshape(1, C), (1, max(tc // C, 1)))
    nbias = bias_t.shape[1] // tc
    mat = pl.BlockSpec((R // 2, R), lambda j: (0, 0))
    big = pl.BlockSpec((R, tc), lambda j: (0, j))
    half = pl.BlockSpec((R // 2, tc), lambda j: (0, j))
    return pl.pallas_call(
        _idft_kernel,
        grid=(W // tc,),
        in_specs=[mat, mat, big, big, half, half, pl.BlockSpec((1, tc), lambda j: (0, j % nbias))],
        out_specs=half,
        out_shape=jax.ShapeDtypeStruct((R // 2, W), BF16),
        compiler_params=_params("parallel"),
        name="idft",
    )(out_r, out_i, cr.reshape(R, W), ci.reshape(R, W), u2d, x02d, bias_t)


def _hyena(p, C, conv_w, conv_b, filt, hy_bias):
    T = p.shape[0]
    R = math.isqrt(2 * T)
    assert R * R == 2 * T and R % 2 == 0
    kf = _hyena_filter(T, C, *filt)
    u, x0 = _short_conv_gate(p, C, conv_w, conv_b)
    lead, fwd, inv, out_r, out_i = _dft_tables(R)
    u2d = u.reshape(R // 2, R * C)
    bur, bui = _dft_lead(lead[:, : R // 2], u2d, R)
    bkr, bki = _dft_lead(lead, kf.reshape(R, R * C), R)
    cr, ci = _spectral(fwd, inv, bur, bui, bkr, bki, R, C)
    y = _idft_lead_gate(out_r, out_i, cr, ci, u2d, x0.reshape(R // 2, R * C), hy_bias, R, C)
    return y.reshape(T, C)


def _rms(x, w):
    return x * lax.rsqrt(jnp.mean(x * x, axis=-1, keepdims=True) + EPS) * w


def _merge_kernel(yh_ref, yd_ref, gh_ref, gd_ref, x_ref, wh_ref, wd_ref, wo_ref,
                  npost_ref, g1_ref, npre_ref, sh2_ref, sc2_ref, x1_ref, ht_ref):
    a = jnp.dot(yh_ref[...], wh_ref[...], preferred_element_type=F32)
    b = jnp.dot(yd_ref[...], wd_ref[...], preferred_element_type=F32)
    m = jax.nn.sigmoid(gh_ref[...].astype(F32)) * a + jax.nn.sigmoid(gd_ref[...].astype(F32)) * b
    mix = jnp.dot(m.astype(BF16), wo_ref[...], preferred_element_type=F32)
    x1 = x_ref[...] + g1_ref[...] * _rms(mix, npost_ref[...])
    x1_ref[...] = x1
    h = _rms(x1, npre_ref[...]) * (1.0 + sc2_ref[...]) + sh2_ref[...]
    ht_ref[...] = h.T.astype(ht_ref.dtype)


def _merge(y_hy, y_da, p, gate_off, x, w_hy, w_da, w_o, npost, g1, npre, sh2, sc2):
    T, D = x.shape
    tm = _pick(T, 256)
    assert gate_off % D == 0
    gb = gate_off // D
    row = lambda w: pl.BlockSpec((tm, w), lambda i: (i, 0))
    full = lambda a: pl.BlockSpec(a.shape, lambda i: (0, 0))
    vec = pl.BlockSpec((1, D), lambda i: (0, 0))
    return pl.pallas_call(
        _merge_kernel,
        grid=(T // tm,),
        in_specs=[row(y_hy.shape[1]), row(y_da.shape[1]),
                  pl.BlockSpec((tm, D), lambda i: (i, gb)), pl.BlockSpec((tm, D), lambda i: (i, gb + 1)),
                  row(D), full(w_hy), full(w_da), full(w_o), vec, vec, vec, vec, vec],
        out_specs=[row(D), pl.BlockSpec((D, tm), lambda i: (0, i))],
        out_shape=[jax.ShapeDtypeStruct((T, D), F32), jax.ShapeDtypeStruct((D, T), BF16)],
        compiler_params=_params("parallel"),
        name="merge",
    )(y_hy, y_da, p, p, x, w_hy, w_da, w_o, npost, g1, npre, sh2, sc2)


def _top_values(s, k):
    vals = []
    for _ in range(k):
        m = jnp.max(s, axis=0, keepdims=True)
        vals.append(m)
        s = jnp.where(s >= m, NEG_INF, s)
    return vals


def _route_kernel(ht_ref, wq_ref, keys_ref, tab_ref, q_scr, *, heads):
    nk = keys_ref.shape[2]
    dk = keys_ref.shape[3]
    q_scr[...] = jnp.dot(wq_ref[...], ht_ref[...], preferred_element_type=F32).astype(BF16)

    def head(h, carry):
        base = pl.multiple_of(h * 2 * dk, 2 * dk)
        s0 = jnp.dot(keys_ref[0, h], q_scr[pl.ds(base, dk), :], preferred_element_type=F32)
        s1 = jnp.dot(keys_ref[1, h], q_scr[pl.ds(base + dk, dk), :], preferred_element_type=F32)
        a = _top_values(s0, PEER_TOPK + 1)
        b = _top_values(s1, PEER_TOPK + 1)
        b_lo = jnp.concatenate(b[:8], axis=0)
        b_hi = jnp.concatenate(b[8:16], axis=0)
        a_hi = jnp.concatenate(a[8:16], axis=0)
        cand = ([a[0] + b_lo, a[0] + b_hi] + [a[r] + b_lo for r in range(1, 8)]
                + [a_hi + b[0], a[0] + b[16], a[16] + b[0]])
        cnd = jnp.concatenate(cand, axis=0)
        kf = float(PEER_TOPK)
        cum = prev = jnp.zeros_like(a[0])
        z = t_k = t_next = jnp.zeros_like(a[0])
        mx = a[0] + b[0]
        for r in range(PEER_TOPK + 1):
            m = jnp.max(cnd, axis=0, keepdims=True)
            eq = cnd >= m
            cnt = jnp.sum(jnp.where(eq, 1.0, 0.0), axis=0, keepdims=True)
            cnd = jnp.where(eq, NEG_INF, cnd)
            new = cum + cnt
            z = z + jnp.clip(kf - cum, 0.0, cnt) * jnp.exp(m - mx)
            t_k = jnp.where(cum < kf, jnp.where(new >= kf, m, t_k), t_k)
            t_next = jnp.where(cum >= kf, jnp.where(prev < kf, m, t_next), t_next)
            prev, cum = cum, new
        tau = 0.5 * (t_k + t_next)
        tab_ref[h, 0] = s0
        tab_ref[h, 1] = jnp.exp(s0 - a[0]) / z
        tab_ref[h, 2] = tau - s1
        tab_ref[h, 3] = jnp.exp(s1 - b[0])
        return carry

    lax.fori_loop(0, heads, head, 0)


def _route(ht, wq_t, keys):
    D, T = ht.shape
    _, heads, nk, dk = keys.shape
    tn = _pick(T, 512)
    return pl.pallas_call(
        functools.partial(_route_kernel, heads=heads),
        grid=(T // tn,),
        in_specs=[pl.BlockSpec((D, tn), lambda i: (0, i)),
                  pl.BlockSpec(wq_t.shape, lambda i: (0, 0)),
                  pl.BlockSpec(keys.shape, lambda i: (0, 0, 0, 0))],
        out_specs=pl.BlockSpec((heads, 4, nk, tn), lambda i: (0, 0, 0, i)),
        out_shape=jax.ShapeDtypeStruct((heads, 4, nk, T), F32),
        scratch_shapes=[pltpu.VMEM((wq_t.shape[0], tn), BF16)],
        compiler_params=_params("parallel"),
        name="route",
    )(ht, wq_t, keys)


def _gelu(x):
    return 0.5 * x * (1.0 + lax.erf(x * (2.0 ** -0.5)))


def _gates_kernel(ht_ref, u_ref, tab_ref, g_ref, *, heads):
    e = pl.program_id(1)
    te = u_ref.shape[0]
    nk = tab_ref.shape[2]
    tn = ht_ref.shape[1]
    act = _gelu(jnp.dot(u_ref[...], ht_ref[...], preferred_element_type=F32))
    for ii in range(te // nk):
        i = e * (te // nk) + ii
        w = jnp.zeros((nk, tn), F32)
        for h in range(heads):
            s0 = tab_ref[h, 0, pl.ds(i, 1), :]
            e0 = tab_ref[h, 1, pl.ds(i, 1), :]
            w = w + jnp.where(s0 >= tab_ref[h, 2], e0 * tab_ref[h, 3], 0.0)
        g_ref[:, ii * nk:(ii + 1) * nk] = (w * act[ii * nk:(ii + 1) * nk]).T.astype(g_ref.dtype)


def _gates(ht, u_bf, tab):
    D, T = ht.shape
    NE = u_bf.shape[0]
    heads, _, nk, _ = tab.shape
    tn = _pick(T, 512)
    te = _pick(NE, 512)
    return pl.pallas_call(
        functools.partial(_gates_kernel, heads=heads),
        grid=(T // tn, NE // te),
        in_specs=[pl.BlockSpec((D, tn), lambda i, e: (0, i)),
                  pl.BlockSpec((te, D), lambda i, e: (e, 0)),
                  pl.BlockSpec((heads, 4, nk, tn), lambda i, e: (0, 0, 0, i))],
        out_specs=pl.BlockSpec((tn, te), lambda i, e: (i, e)),
        out_shape=jax.ShapeDtypeStruct((T, NE), BF16),
        compiler_params=_params("parallel", "arbitrary"),
        name="gates",
    )(ht, u_bf, tab)


def _combine_kernel(g_ref, v_ref, x1_ref, nw_ref, g2_ref, o_ref, acc_scr):
    k = pl.program_id(1)

    @pl.when(k == 0)
    def _():
        acc_scr[...] = jnp.zeros(acc_scr.shape, F32)

    acc_scr[...] += jnp.dot(g_ref[...], v_ref[...], preferred_element_type=F32)

    @pl.when(k == pl.num_programs(1) - 1)
    def _():
        o_ref[...] = x1_ref[...] + g2_ref[...] * _rms(acc_scr[...], nw_ref[...])


def _combine(g, v_bf, x1, nw, g2):
    T, NE = g.shape
    D = v_bf.shape[1]
    tm = _pick(T, 512)
    tk = _pick(NE, 1024)
    row = pl.BlockSpec((tm, D), lambda i, k: (i, 0))
    vec = pl.BlockSpec((1, D), lambda i, k: (0, 0))
    return pl.pallas_call(
        _combine_kernel,
        grid=(T // tm, NE // tk),
        in_specs=[pl.BlockSpec((tm, tk), lambda i, k: (i, k)),
                  pl.BlockSpec((tk, D), lambda i, k: (k, 0)), row, vec, vec],
        out_specs=row,
        out_shape=jax.ShapeDtypeStruct((T, D), F32),
        scratch_shapes=[pltpu.VMEM((tm, D), F32)],
        compiler_params=_params("parallel", "arbitrary"),
        name="combine",
    )(g, v_bf, x1, nw, g2)


def kernel(x, c, ctx, c_ctx, w_mod, b_mod, norm_pre_mix, norm_post_mix, norm_pre_ffn, norm_post_ffn,
           w_in, hy_conv_w, hy_conv_b, filt_w1, filt_b1, filt_w2, filt_b2, filt_w3, filt_b3,
           filt_w_out, filt_freq, hy_bias, da_lambda, da_subln, w_hy_out, w_da_out, w_o,
           peer_w_q, peer_keys, peer_u, peer_v):
    B, T, D = x.shape
    assert B == 1 and w_mod.shape[0] == 1, "single batch, single layer"
    C = hy_bias.shape[-1]
    in_cols = w_in.shape[-1]
    heads = (in_cols - 3 * C - 2 * D) // (3 * LANES)
    q_off = 3 * C
    k_off = q_off + heads * LANES
    gate_off = k_off + 2 * heads * LANES
    vec = lambda a: a.reshape(1, D)

    mod = _modulation(jnp.concatenate([c, c_ctx[None]], axis=0), w_mod[0], b_mod[0]).reshape(2, N_MOD, D)
    sh1, sc1, g1, sh2, sc2, g2 = [mod[0, i][None] for i in range(N_MOD)]
    csh1, csc1 = mod[1, 0][None], mod[1, 1][None]

    w_in_bf = w_in[0].astype(BF16)
    p = _project(x[0], vec(norm_pre_mix[0]), sh1, sc1, w_in_bf, 0, in_cols)
    p_ctx = _project(ctx[0], vec(norm_pre_mix[0]), csh1, csc1, w_in_bf, k_off, 2 * heads * LANES)

    filt = (filt_w1[0], filt_b1[0], filt_w2[0], filt_b2[0], filt_w3[0], filt_b3[0], filt_w_out[0], filt_freq[0])
    y_hy = _hyena(p, C, hy_conv_w[0], hy_conv_b[0], filt, hy_bias[0])

    q, k_all, v_all = _rope_assemble(p, p_ctx, heads, q_off)
    y_da = _diff_attention(q, k_all, v_all, da_lambda[0], da_subln[0], heads)

    x1, ht = _merge(y_hy, y_da, p, gate_off, x[0], w_hy_out[0].astype(BF16), w_da_out[0].astype(BF16),
                    w_o[0].astype(BF16), vec(norm_post_mix[0]), g1, vec(norm_pre_ffn[0]), sh2, sc2)

    tab = _route(ht, peer_w_q[0].T.astype(BF16), peer_keys[0].astype(BF16))
    gates = _gates(ht, peer_u[0].astype(BF16), tab)
    return _combine(gates, peer_v[0].astype(BF16), x1, vec(norm_post_ffn[0]), g2)[None]
```
